```python
import jax
import jax.numpy as jnp
from jax import lax
import numpy as np

D_MODEL = 1024
BATCH = 4
SEQ = 8192
DEPTH = 2

N_A_LAYERS = DEPTH // 2
N_B_LAYERS = DEPTH - N_A_LAYERS
HEAD_DIM = 64
N_HEADS = D_MODEL // HEAD_DIM
DECAY_LORA = 64
AAA_LORA = 64
GATE_LORA = 128
GN_EPS = 64e-5
LN_EPS = 1e-5
QK_EPS = 1e-6
Q_BLOCK = 128
SCORE_SCALE = HEAD_DIM ** -0.5
N_EXPERTS = 256
TOP_K = 8
N_GROUPS = 8
TOPK_GROUPS = 4
D_EXPERT = 256
ROUTE_SCALE = 2.5
EXPERT_BLOCK = 64
ALPHA = (2.0 * DEPTH) ** 0.25
BETA = (8.0 * DEPTH) ** -0.25

kernel_name = 'yoco_rwkv7_fox_moe_block'


def layer_norm(x, g, b):
    xf = x.astype(jnp.float32)
    mu = jnp.mean(xf, -1, keepdims=True)
    var = jnp.mean(jnp.square(xf - mu), -1, keepdims=True)
    return ((xf - mu) * lax.rsqrt(var + LN_EPS) * g + b).astype(x.dtype)


def rms_norm(x, g):
    xf = x.astype(jnp.float32)
    return (xf * lax.rsqrt(jnp.mean(xf * xf, -1, keepdims=True) + QK_EPS) * g).astype(x.dtype)


def swiglu(x, w_gate, w_up, w_down):
    return (jax.nn.silu(x @ w_gate) * (x @ w_up)) @ w_down


def rwkv7_recurrence(r, decay, k, v, a, b):
    bsz, _, h, n = r.shape
    def step(S, inp):
        r_t, w_t, k_t, v_t, a_t, b_t = inp
        sa = jnp.einsum('bhvk,bhk->bhv', S, a_t)
        S = S * w_t[:, :, None, :] + sa[..., None] * b_t[:, :, None, :] + v_t[..., None] * k_t[:, :, None, :]
        return S, jnp.einsum('bhvk,bhk->bhv', S, r_t)
    S0 = jnp.zeros((bsz, h, n, n), jnp.float32)
    xs = tuple(jnp.moveaxis(t.astype(jnp.float32), 1, 0) for t in (r, decay, k, v, a, b))
    _, y = lax.scan(step, S0, xs)
    return jnp.moveaxis(y, 0, 1)


def rwkv7_time_mix(h, mu, w_rkv, w_o, w0, w1, w2, a0, a1, a2, g1, g2, k_k, k_a, r_k, lnx_g, lnx_b):
    bsz, t, d = h.shape
    heads = lambda z: z.reshape(bsz, t, N_HEADS, HEAD_DIM)
    xx = jnp.pad(h, ((0, 0), (1, 0), (0, 0)))[:, :-1] - h
    xr, xw, xk, xv, xa, xg = (h + xx * mu[i] for i in range(6))
    r = xr @ w_rkv[0]
    k = xk @ w_rkv[1]
    v = xv @ w_rkv[2]
    w = -jax.nn.softplus(-(w0 + jnp.tanh(xw @ w1) @ w2)) - 0.5
    a = jax.nn.sigmoid(a0 + (xa @ a1) @ a2)
    g = jax.nn.sigmoid(xg @ g1) @ g2
    kk = heads(k * k_k).astype(jnp.float32)
    kk = kk * lax.rsqrt(jnp.maximum(jnp.sum(kk * kk, -1, keepdims=True), 1e-24))
    k = k * (1 + (a - 1) * k_a)
    decay = jnp.exp(-jnp.exp(w.astype(jnp.float32)))
    y = rwkv7_recurrence(heads(r), heads(decay), heads(k), heads(v), -kk, kk * heads(a).astype(jnp.float32))
    mean = jnp.mean(y, -1, keepdims=True)
    var = jnp.mean(jnp.square(y - mean), -1, keepdims=True)
    y = ((y - mean) * lax.rsqrt(var + GN_EPS)).reshape(bsz, t, d) * lnx_g + lnx_b
    bonus = jnp.sum(heads(r) * heads(k) * r_k, -1, keepdims=True) * heads(v)
    y = (y.astype(h.dtype) + bonus.reshape(bsz, t, d)) * g
    return y @ w_o


def shared_kv(x_kv, kv_w, f_b, k_norm_g):
    bsz, t, d = x_kv.shape
    proj = x_kv @ kv_w
    k = rms_norm(proj[..., :d].reshape(bsz, t, N_HEADS, HEAD_DIM), k_norm_g)
    v = proj[..., d:2 * d].reshape(bsz, t, N_HEADS, HEAD_DIM)
    log_f = jax.nn.log_sigmoid((proj[..., 2 * d:] + f_b).astype(jnp.float32))
    F = jnp.cumsum(log_f, axis=1)
    return k, v, F


def fox_attention(h, w_qg, q_norm_g, w_o, k, v, F):
    bsz, t, d = h.shape
    nb = t // Q_BLOCK
    qg = h @ w_qg
    q = rms_norm(qg[..., :d].reshape(bsz, t, N_HEADS, HEAD_DIM), q_norm_g)
    gate = jax.nn.sigmoid(qg[..., d:])
    q_blocks = q.reshape(bsz, nb, Q_BLOCK, N_HEADS, HEAD_DIM).transpose(1, 0, 3, 2, 4).astype(jnp.float32)
    F_blocks = F.reshape(bsz, nb, Q_BLOCK, N_HEADS).transpose(1, 0, 3, 2)
    kf = k.transpose(0, 2, 1, 3).astype(jnp.float32)
    vf = v.transpose(0, 2, 1, 3).astype(jnp.float32)
    Fk = F.transpose(0, 2, 1)
    kpos = jnp.arange(t)
    def block(args):
        i, q_i, F_i = args
        s = jnp.einsum('bhqd,bhkd->bhqk', q_i, kf) * SCORE_SCALE + (F_i[..., None] - Fk[:, :, None, :])
        qpos = i * Q_BLOCK + jnp.arange(Q_BLOCK)
        s = jnp.where(kpos[None, :] <= qpos[:, None], s, -jnp.inf)
        p = jax.nn.softmax(s, axis=-1)
        return jnp.einsum('bhqk,bhkd->bhqd', p, vf)
    o = lax.map(block, (jnp.arange(nb), q_blocks, F_blocks))
    o = o.transpose(1, 0, 3, 2, 4).reshape(bsz, t, d).astype(h.dtype)
    return (o * gate) @ w_o


def routed_experts(h, router, router_bias, w_gate, w_up, w_down):
    t, d = h.shape
    n = t * TOP_K
    scores = jax.nn.sigmoid((h @ router).astype(jnp.float32))
    choice = scores + router_bias.astype(jnp.float32)
    grp = choice.reshape(t, N_GROUPS, N_EXPERTS // N_GROUPS)
    grp_score = jnp.sum(lax.top_k(grp, 2)[0], -1)
    _, top_grp = lax.top_k(grp_score, TOPK_GROUPS)
    grp_mask = jnp.sum(jax.nn.one_hot(top_grp, N_GROUPS, dtype=jnp.float32), 1) > 0
    expert_mask = jnp.repeat(grp_mask, N_EXPERTS // N_GROUPS, axis=1)
    _, idx = lax.top_k(jnp.where(expert_mask, choice, -jnp.inf), TOP_K)
    gw = jnp.take_along_axis(scores, idx, axis=1)
    gw = gw / jnp.sum(gw, -1, keepdims=True) * ROUTE_SCALE
    flat_e = idx.reshape(n)
    order = jnp.argsort(flat_e)
    e_sorted = flat_e[order]
    tok = order // TOP_K
    counts = jnp.bincount(flat_e, length=N_EXPERTS)
    padded = (counts + EXPERT_BLOCK - 1) // EXPERT_BLOCK * EXPERT_BLOCK
    pad_end = jnp.cumsum(padded)
    pad_start = pad_end - padded
    start = jnp.cumsum(counts) - counts
    dest = pad_start[e_sorted] + jnp.arange(n) - start[e_sorted]
    n_blocks = (n + N_EXPERTS * (EXPERT_BLOCK - 1) + EXPERT_BLOCK - 1) // EXPERT_BLOCK
    buf = jnp.zeros((n_blocks * EXPERT_BLOCK, d), h.dtype).at[dest].set(h[tok])
    block_expert = jnp.minimum(jnp.searchsorted(pad_end, jnp.arange(n_blocks) * EXPERT_BLOCK, side='right'), N_EXPERTS - 1)
    def expert_block(args):
        xb, e = args
        return swiglu(xb, w_gate[e], w_up[e], w_down[e])
    out = lax.map(expert_block, (buf.reshape(n_blocks, EXPERT_BLOCK, d), block_expert))
    out = out.reshape(n_blocks * EXPERT_BLOCK, d)[dest]
    weights = gw.reshape(n)[order].astype(h.dtype)
    return jnp.zeros((t, d), h.dtype).at[tok].add(out * weights[:, None])


def moe_ffn(h, router, router_bias, w_gate, w_up, w_down, s_gate, s_up, s_down):
    routed = lax.map(lambda hb: routed_experts(hb, router, router_bias, w_gate, w_up, w_down), h)
    return routed + swiglu(h, s_gate, s_up, s_down)


def setup_inputs(seed: int = 0) -> dict:
    key = jax.random.key(seed)
    ks = iter(jax.random.split(key, 40))
    nrm = lambda shape, scale: jax.random.normal(next(ks), shape, jnp.float32) * scale
    D, H, N, E, F = D_MODEL, N_HEADS, HEAD_DIM, N_EXPERTS, D_EXPERT
    A, Bn, L = N_A_LAYERS, N_B_LAYERS, DEPTH
    s = D ** -0.5
    return {
        'x': nrm((BATCH, SEQ, D), 1.0),
        'c': nrm((BATCH, D), 1.0),
        'ada_w': nrm((L, D, 6 * D), 0.1 * s),
        'ada_b': nrm((L, 6 * D), 0.02),
        'ln_g': 1.0 + nrm((L, 2, D), 0.02),
        'ln_b': nrm((L, 2, D), 0.02),
        'rwkv_mu': jax.random.uniform(next(ks), (A, 6, D), jnp.float32),
        'rwkv_w_rkv': nrm((A, 3, D, D), s),
        'rwkv_w_o': nrm((A, D, D), s * BETA),
        'rwkv_w0': nrm((A, D), 0.5),
        'rwkv_w1': nrm((A, D, DECAY_LORA), s),
        'rwkv_w2': nrm((A, DECAY_LORA, D), 0.1 * DECAY_LORA ** -0.5),
        'rwkv_a0': nrm((A, D), 0.1),
        'rwkv_a1': nrm((A, D, AAA_LORA), s),
        'rwkv_a2': nrm((A, AAA_LORA, D), 0.1 * AAA_LORA ** -0.5),
        'rwkv_g1': nrm((A, D, GATE_LORA), s),
        'rwkv_g2': nrm((A, GATE_LORA, D), GATE_LORA ** -0.5),
        'rwkv_k_k': 0.85 + nrm((A, D), 0.02),
        'rwkv_k_a': 1.0 + nrm((A, D), 0.02),
        'rwkv_r_k': nrm((A, H, N), 0.1),
        'rwkv_lnx_g': 1.0 + nrm((A, D), 0.02),
        'rwkv_lnx_b': nrm((A, D), 0.02),
        'kv_w': nrm((D, 2 * D + H), s),
        'f_b': 2.0 + nrm((H,), 0.1),
        'k_norm_g': 1.0 + nrm((N,), 0.02),
        'fox_w_qg': nrm((Bn, D, 2 * D), s),
        'fox_q_norm_g': 1.0 + nrm((Bn, N), 0.02),
        'fox_w_o': nrm((Bn, D, D), s * BETA),
        'moe_router': nrm((L, D, E), s),
        'moe_router_bias': nrm((L, E), 0.01),
        'moe_w_gate': nrm((L, E, D, F), s),
        'moe_w_up': nrm((L, E, D, F), s),
        'moe_w_down': nrm((L, E, F, D), F ** -0.5 * BETA),
        'shared_w_gate': nrm((L, D, F), s),
        'shared_w_up': nrm((L, D, F), s),
        'shared_w_down': nrm((L, F, D), F ** -0.5 * BETA),
    }


def reference(x, c, ada_w, ada_b, ln_g, ln_b, rwkv_mu, rwkv_w_rkv, rwkv_w_o, rwkv_w0, rwkv_w1, rwkv_w2,
              rwkv_a0, rwkv_a1, rwkv_a2, rwkv_g1, rwkv_g2, rwkv_k_k, rwkv_k_a, rwkv_r_k, rwkv_lnx_g, rwkv_lnx_b,
              kv_w, f_b, k_norm_g, fox_w_qg, fox_q_norm_g, fox_w_o, moe_router, moe_router_bias,
              moe_w_gate, moe_w_up, moe_w_down, shared_w_gate, shared_w_up, shared_w_down):
    mod = jnp.einsum('bd,lde->lbe', jax.nn.silu(c), ada_w) + ada_b[:, None, :]
    k_sh = v_sh = F_sh = None
    for l in range(DEPTH):
        shift_t, scale_t, gate_t, shift_c, scale_c, gate_c = (m[:, None, :] for m in jnp.split(mod[l], 6, axis=-1))
        h = x * (1 + scale_t) + shift_t
        if l < N_A_LAYERS:
            y = rwkv7_time_mix(h, rwkv_mu[l], rwkv_w_rkv[l], rwkv_w_o[l], rwkv_w0[l], rwkv_w1[l], rwkv_w2[l],
                               rwkv_a0[l], rwkv_a1[l], rwkv_a2[l], rwkv_g1[l], rwkv_g2[l], rwkv_k_k[l],
                               rwkv_k_a[l], rwkv_r_k[l], rwkv_lnx_g[l], rwkv_lnx_b[l])
        else:
            j = l - N_A_LAYERS
            if j == 0:
                k_sh, v_sh, F_sh = shared_kv(x, kv_w, f_b, k_norm_g)
            y = fox_attention(h, fox_w_qg[j], fox_q_norm_g[j], fox_w_o[j], k_sh, v_sh, F_sh)
        x = layer_norm(ALPHA * x + (1 + gate_t) * y, ln_g[l, 0], ln_b[l, 0])
        h = x * (1 + scale_c) + shift_c
        y = moe_ffn(h, moe_router[l], moe_router_bias[l], moe_w_gate[l], moe_w_up[l], moe_w_down[l],
                    shared_w_gate[l], shared_w_up[l], shared_w_down[l])
        x = layer_norm(ALPHA * x + (1 + gate_c) * y, ln_g[l, 1], ln_b[l, 1])
    return x
```

```python
import functools

import jax
import jax.numpy as jnp
from jax import lax
from jax.experimental import pallas as pl
from jax.experimental.pallas import tpu as pltpu

f32 = jnp.float32
bf16 = jnp.bfloat16
i32 = jnp.int32

D_MODEL = 1024
HEAD_DIM = 64
N_HEADS = D_MODEL // HEAD_DIM
N_EXPERTS = 256
D_EXPERT = 256
TOP_K = 8
N_GROUPS = 8
TOPK_GROUPS = 4
GROUP_SIZE = N_EXPERTS // N_GROUPS
ROUTE_SCALE = 2.5
DEPTH = 2
GN_EPS = 64e-5
LN_EPS = 1e-5
QK_EPS = 1e-6
SCORE_SCALE = HEAD_DIM ** -0.5
ALPHA = (2.0 * DEPTH) ** 0.25

LANES = 128
TOKEN_TILE = 256
REC_CHUNK = 64
REC_BLOCK = 256
REC_LANES = 256
EXPERT_ROWS = 256
ATT_Q = 256
ATT_K = 512
ATT_LANES = 128
VMEM_LIMIT = 56 << 20


def _cp(sem):
    return pltpu.CompilerParams(dimension_semantics=sem, vmem_limit_bytes=VMEM_LIMIT)


def _dot(a, b):
    return jnp.dot(a, b, preferred_element_type=f32)


def _dot_nt(a, b):
    return lax.dot_general(a, b, (((1,), (1,)), ((), ())), preferred_element_type=f32)


def _dot_tn(a, b):
    return lax.dot_general(a, b, (((0,), (0,)), ((), ())), preferred_element_type=f32)


def _split2(z):
    hi = z.astype(bf16)
    lo = (z - hi.astype(f32)).astype(bf16)
    return hi, lo


def _split3(z):
    hi = z.astype(bf16)
    r1 = z - hi.astype(f32)
    mid = r1.astype(bf16)
    lo = (r1 - mid.astype(f32)).astype(bf16)
    return hi, mid, lo


def _dot_exact_rhs(z, m):
    hi, mid, lo = _split3(z)
    return _dot(hi, m) + _dot(mid, m) + _dot(lo, m)


def _dot_exact_lhs(m, z):
    hi, mid, lo = _split3(z)
    return _dot(m, hi) + _dot(m, mid) + _dot(m, lo)


def _dot3(a, b, dot=_dot):
    ah, al = _split2(a)
    bh, bl = _split2(b)
    return dot(ah, bh) + dot(ah, bl) + dot(al, bh)


def _segsum(z, eh, eht):
    s = _dot_exact_rhs(z, eh)
    return _dot_exact_rhs(s, eht)


def _sigmoid(z):
    return 1.0 / (1.0 + jnp.exp(-z))


def _softplus(z):
    return jnp.maximum(z, 0.0) + jnp.log(1.0 + jnp.exp(-jnp.abs(z)))


def _layer_norm(z, g, b):
    mu = jnp.mean(z, axis=-1, keepdims=True)
    d = z - mu
    var = jnp.mean(d * d, axis=-1, keepdims=True)
    return d * lax.rsqrt(var + LN_EPS) * g + b


def _head_indicator():
    d = jnp.arange(D_MODEL)[:, None] // HEAD_DIM
    j = jnp.arange(LANES)[None, :]
    eh = (d == j).astype(bf16)
    return eh, eh.T


def _full(shape):
    n = len(shape)
    return pl.BlockSpec(shape, lambda *_: (0,) * n)


def _mod_kernel(c_ref, w_ref, b_ref, o_ref):
    c = c_ref[...]
    sc = c * _sigmoid(c)
    o_ref[...] = jnp.dot(sc, w_ref[...], precision=lax.Precision.HIGHEST,
                         preferred_element_type=f32) + b_ref[...]


def _mod(c, ada_w, ada_b):
    L, D, D6 = ada_w.shape
    B = c.shape[0]
    nj = D6 // D
    out = pl.pallas_call(
        _mod_kernel,
        grid=(L, nj),
        in_specs=[
            pl.BlockSpec((B, D), lambda l, j: (0, 0)),
            pl.BlockSpec((None, D, D), lambda l, j: (l, 0, j)),
            pl.BlockSpec((None, 1, D), lambda l, j: (l, 0, j)),
        ],
        out_specs=pl.BlockSpec((None, B, D), lambda l, j: (l, 0, j)),
        out_shape=jax.ShapeDtypeStruct((L, B, D6), f32),
        compiler_params=_cp(("arbitrary", "arbitrary")),
        name="mod",
    )(c, ada_w, ada_b.reshape(L, 1, D6))
    return out.reshape(L, B, nj, D)


def _rwkv_pre_kernel(x_ref, mod_ref, mu_ref, wrkv_ref, w1_ref, w2_ref, a1_ref, a2_ref,
                     g1_ref, g2_ref, vec_ref, eh_ref, eht_ref,
                     r_o, lw_o, k_o, v_o, kk_o, b_o, bonus_o, g_o, prev_ref):
    tm = x_ref.shape[0]

    @pl.when(pl.program_id(1) == 0)
    def _():
        prev_ref[...] = jnp.zeros_like(prev_ref)

    h = x_ref[...] * (1.0 + mod_ref[1:2, :]) + mod_ref[0:1, :]
    row = lax.broadcasted_iota(i32, h.shape, 0)
    hprev = jnp.where(row == 0, prev_ref[...], pltpu.roll(h, 1, axis=0))
    prev_ref[...] = h[tm - 1:tm, :]
    xx = hprev - h

    def mix(j):
        return (h + xx * mu_ref[j:j + 1, :]).astype(bf16)

    r = _dot(mix(0), wrkv_ref[0])
    k = _dot(mix(2), wrkv_ref[1])
    v = _dot(mix(3), wrkv_ref[2])
    w0, a0, k_k, k_a, r_k = (vec_ref[j:j + 1, :] for j in range(5))
    wl = _dot(jnp.tanh(_dot(mix(1), w1_ref[...])).astype(bf16), w2_ref[...])
    w = -_softplus(-(w0 + wl)) - 0.5
    a = _sigmoid(a0 + _dot(_dot(mix(4), a1_ref[...]).astype(bf16), a2_ref[...]))
    g = _dot(_sigmoid(_dot(mix(5), g1_ref[...])).astype(bf16), g2_ref[...])
    eh = eh_ref[...]
    eht = eht_ref[...]
    kkr = k * k_k
    kk = kkr * lax.rsqrt(jnp.maximum(_segsum(kkr * kkr, eh, eht), 1e-24))
    k2 = k * (1.0 + (a - 1.0) * k_a)
    r_o[...] = r
    lw_o[...] = -jnp.exp(w)
    k_o[...] = k2
    v_o[...] = v
    kk_o[...] = kk
    b_o[...] = kk * a
    bonus_o[...] = _segsum(r * k2 * r_k, eh, eht) * v
    g_o[...] = g


def _rwkv_pre(x, mod_l, mu, w_rkv, w1, w2, a1, a2, g1, g2, vecs, eh, eht):
    B, T, D = x.shape
    tm = min(TOKEN_TILE, T)
    tok = pl.BlockSpec((None, tm, D), lambda b, i: (b, i, 0))
    outs = pl.pallas_call(
        _rwkv_pre_kernel,
        grid=(B, T // tm),
        in_specs=[
            tok,
            pl.BlockSpec((None, 6, D), lambda b, i: (b, 0, 0)),
            _full(mu.shape), _full(w_rkv.shape), _full(w1.shape), _full(w2.shape),
            _full(a1.shape), _full(a2.shape), _full(g1.shape), _full(g2.shape),
            _full(vecs.shape), _full(eh.shape), _full(eht.shape),
        ],
        out_specs=[tok] * 8,
        out_shape=[jax.ShapeDtypeStruct((B, T, D), f32)] * 8,
        scratch_shapes=[pltpu.VMEM((1, D), f32)],
        compiler_params=_cp(("arbitrary", "arbitrary")),
        name="rwkv_pre",
    )(x, mod_l, mu, w_rkv, w1, w2, a1, a2, g1, g2, vecs, eh, eht)
    return outs


def _rwkv_rec_kernel(r_ref, lw_ref, k_ref, v_ref, kk_ref, b_ref, y_ref, st_ref):
    C = REC_CHUNK
    W = REC_LANES
    G = W // HEAD_DIM
    n_chunks = r_ref.shape[0] // C

    @pl.when(pl.program_id(2) == 0)
    def _():
        st_ref[...] = jnp.zeros_like(st_ref)

    er = lax.broadcasted_iota(i32, (G * C, W), 0) // C
    el = lax.broadcasted_iota(i32, (G * C, W), 1) // HEAD_DIM
    bd_exp = er == el
    mt = lax.broadcasted_iota(i32, (C, G * C), 0)
    mi = lax.broadcasted_iota(i32, (C, G * C), 1) % C
    strict = mi < mt
    incl = mi <= mt
    eye = (mi == mt).astype(f32)
    sr = lax.broadcasted_iota(i32, (W, W), 0) // HEAD_DIM
    sl_ = lax.broadcasted_iota(i32, (W, W), 1) // HEAD_DIM
    bd_state = sr == sl_
    tri = (lax.broadcasted_iota(i32, (C, C), 1) <= lax.broadcasted_iota(i32, (C, C), 0)).astype(bf16)

    def expand(z):
        return jnp.where(bd_exp, jnp.concatenate([z] * G, axis=0), 0.0)

    for c in range(n_chunks):
        rows = slice(c * C, (c + 1) * C)
        r = r_ref[rows, :]
        lw = lw_ref[rows, :]
        k = k_ref[rows, :]
        v = v_ref[rows, :]
        kk = kk_ref[rows, :]
        b = b_ref[rows, :]
        cw = _dot_exact_lhs(tri, lw)
        cwp = cw - lw
        cwe = cw[C - 1:C, :]
        rt = r * jnp.exp(cw)
        at = -kk * jnp.exp(cwp)
        en = jnp.exp(-cw)
        bt = b * en
        kt = k * en
        ec = jnp.exp(cwe - cw)
        b2 = b * ec
        k2 = k * ec
        vh = expand(v)
        aa = _dot3(jnp.concatenate([at, rt], axis=0),
                   jnp.concatenate([expand(bt), expand(kt)], axis=0), _dot_nt)
        aab = jnp.where(strict, aa[:C, :G * C], 0.0)
        aak = jnp.where(strict, aa[:C, G * C:], 0.0)
        arb = jnp.where(incl, aa[C:, :G * C], 0.0)
        ark = jnp.where(incl, aa[C:, G * C:], 0.0)
        tinv = eye + aab
        p = aab
        for _ in range(5):
            p = _dot3(p, expand(p))
            tinv = tinv + _dot3(tinv, expand(p))
        w0 = _dot3(aak, vh)
        yv = _dot3(ark, vh)
        st = st_ref[...]
        wv = _dot3(at, st, _dot_nt) + w0
        u = _dot3(tinv, expand(wv))
        y_ref[rows, :] = _dot3(rt, st, _dot_nt) + _dot3(arb, expand(u)) + yv
        upd = _dot3(jnp.concatenate([u, v], axis=0), jnp.concatenate([b2, k2], axis=0), _dot_tn)
        st_ref[...] = st * jnp.exp(cwe) + jnp.where(bd_state, upd, 0.0)


def _rwkv_rec(r, lw, k, v, kk, b):
    B, T, D = r.shape
    tb = min(REC_BLOCK, T)
    blk = pl.BlockSpec((None, tb, REC_LANES), lambda bi, q, i: (bi, i, q))
    return pl.pallas_call(
        _rwkv_rec_kernel,
        grid=(B, D // REC_LANES, T // tb),
        in_specs=[blk] * 6,
        out_specs=blk,
        out_shape=jax.ShapeDtypeStruct((B, T, D), f32),
        scratch_shapes=[pltpu.VMEM((REC_LANES, REC_LANES), f32)],
        compiler_params=_cp(("arbitrary", "arbitrary", "arbitrary")),
        name="rwkv_rec",
    )(r, lw, k, v, kk, b)


def _rwkv_post_kernel(y_ref, bonus_ref, g_ref, x_ref, mod_ref, vec_ref, wo_ref, eh_ref, eht_ref, o_ref):
    eh = eh_ref[...]
    eht = eht_ref[...]
    y = y_ref[...]
    mean = _segsum(y, eh, eht) * (1.0 / HEAD_DIM)
    d = y - mean
    var = _segsum(d * d, eh, eht) * (1.0 / HEAD_DIM)
    yn = d * lax.rsqrt(var + GN_EPS) * vec_ref[0:1, :] + vec_ref[1:2, :]
    z = (yn + bonus_ref[...]) * g_ref[...]
    o = _dot(z.astype(bf16), wo_ref[...])
    xn = ALPHA * x_ref[...] + (1.0 + mod_ref[2:3, :]) * o
    o_ref[...] = _layer_norm(xn, vec_ref[2:3, :], vec_ref[3:4, :])


def _rwkv_post(y, bonus, g, x, mod_l, vecs, w_o, eh, eht):
    B, T, D = x.shape
    tm = min(TOKEN_TILE, T)
    tok = pl.BlockSpec((None, tm, D), lambda b, i: (b, i, 0))
    return pl.pallas_call(
        _rwkv_post_kernel,
        grid=(B, T // tm),
        in_specs=[tok, tok, tok, tok,
                  pl.BlockSpec((None, 6, D), lambda b, i: (b, 0, 0)),
                  _full(vecs.shape), _full(w_o.shape), _full(eh.shape), _full(eht.shape)],
        out_specs=tok,
        out_shape=jax.ShapeDtypeStruct((B, T, D), f32),
        compiler_params=_cp(("arbitrary", "arbitrary")),
        name="rwkv_post",
    )(y, bonus, g, x, mod_l, vecs, w_o, eh, eht)


def _moe_route_kernel(x_ref, mod_ref, router_ref, bias_ref, idx_o, gw_o, rank_o, cnt_o, carry_ref):
    tm = x_ref.shape[0]
    E = N_EXPERTS

    @pl.when((pl.program_id(0) == 0) & (pl.program_id(1) == 0))
    def _():
        carry_ref[...] = jnp.zeros_like(carry_ref)

    h = x_ref[...] * (1.0 + mod_ref[4:5, :]) + mod_ref[3:4, :]
    logits = jnp.dot(h, router_ref[...], precision=lax.Precision.HIGHEST, preferred_element_type=f32)
    scores = _sigmoid(logits)
    choice = scores + bias_ref[...]
    lane = lax.broadcasted_iota(i32, (tm, E), 1)
    grp = lane // GROUP_SIZE
    neg = -jnp.inf

    def first_argmax(z):
        m = jnp.max(z, axis=-1, keepdims=True)
        idx = jnp.min(jnp.where(z == m, lane, E), axis=-1, keepdims=True)
        return m, idx

    gs = []
    for gi in range(N_GROUPS):
        zg = jnp.where(grp == gi, choice, neg)
        m1, i1 = first_argmax(zg)
        m2 = jnp.max(jnp.where(lane == i1, neg, zg), axis=-1, keepdims=True)
        gs.append(m1 + m2)
    emask = jnp.zeros((tm, E), jnp.bool_)
    for gi in range(N_GROUPS):
        beaten = jnp.zeros((tm, 1), i32)
        for gj in range(N_GROUPS):
            if gj == gi:
                continue
            ahead = (gs[gj] > gs[gi]) | ((gs[gj] == gs[gi]) & (gj < gi))
            beaten = beaten + ahead.astype(i32)
        emask = emask | ((grp == gi) & (beaten < TOPK_GROUPS))
    mc = jnp.where(emask, choice, neg)
    lane8 = lax.broadcasted_iota(i32, (tm, TOP_K), 1)
    idx_all = jnp.zeros((tm, TOP_K), i32)
    sc_all = jnp.zeros((tm, TOP_K), f32)
    onehot = jnp.zeros((tm, E), f32)
    picks = []
    for kk in range(TOP_K):
        _, ik = first_argmax(mc)
        hit = lane == ik
        sk = jnp.sum(jnp.where(hit, scores, 0.0), axis=-1, keepdims=True)
        mc = jnp.where(hit, neg, mc)
        onehot = onehot + hit.astype(f32)
        idx_all = jnp.where(lane8 == kk, ik, idx_all)
        sc_all = jnp.where(lane8 == kk, sk, sc_all)
        picks.append(hit)
    gw_o[...] = sc_all / jnp.sum(sc_all, axis=-1, keepdims=True) * ROUTE_SCALE
    idx_o[...] = idx_all
    tri = (lax.broadcasted_iota(i32, (tm, tm), 1) < lax.broadcasted_iota(i32, (tm, tm), 0)).astype(bf16)
    before = _dot(tri, onehot.astype(bf16)) + carry_ref[...]
    rank_all = jnp.zeros((tm, TOP_K), f32)
    for kk in range(TOP_K):
        rk = jnp.sum(jnp.where(picks[kk], before, 0.0), axis=-1, keepdims=True)
        rank_all = jnp.where(lane8 == kk, rk, rank_all)
    rank_o[...] = rank_all.astype(i32)
    total = carry_ref[...] + jnp.sum(onehot, axis=0, keepdims=True)
    carry_ref[...] = total
    cnt_o[...] = total.astype(i32)


def _moe_route(x, mod_l, router, bias):
    B, T, D = x.shape
    tm = min(TOKEN_TILE, T)
    tok = pl.BlockSpec((None, tm, D), lambda b, i: (b, i, 0))
    sel = pl.BlockSpec((None, tm, TOP_K), lambda b, i: (b, i, 0))
    return pl.pallas_call(
        _moe_route_kernel,
        grid=(B, T // tm),
        in_specs=[tok, pl.BlockSpec((None, 6, D), lambda b, i: (b, 0, 0)),
                  _full(router.shape), _full(bias.shape)],
        out_specs=[sel, sel, sel, _full((1, N_EXPERTS))],
        out_shape=[jax.ShapeDtypeStruct((B, T, TOP_K), i32),
                   jax.ShapeDtypeStruct((B, T, TOP_K), f32),
                   jax.ShapeDtypeStruct((B, T, TOP_K), i32),
                   jax.ShapeDtypeStruct((1, N_EXPERTS), i32)],
        scratch_shapes=[pltpu.VMEM((1, N_EXPERTS), f32)],
        compiler_params=_cp(("arbitrary", "arbitrary")),
        name="moe_route",
    )(x, mod_l, router, bias)


def _moe_disp_kernel(dest_ref, x_ref, mod_ref, buf_in, buf_o, h_ref, sem):
    del buf_in
    tm = x_ref.shape[0]
    h_ref[...] = x_ref[...] * (1.0 + mod_ref[4:5, :]) + mod_ref[3:4, :]

    def row_copy(t, kk):
        d = dest_ref[t * TOP_K + kk]
        return pltpu.make_async_copy(h_ref.at[pl.ds(t, 1), :], buf_o.at[pl.ds(d, 1), :], sem)

    def start(t, c):
        for kk in range(TOP_K):
            row_copy(t, kk).start()
        return c

    def wait(t, c):
        for kk in range(TOP_K):
            row_copy(t, kk).wait()
        return c

    lax.fori_loop(0, tm, start, 0)
    lax.fori_loop(0, tm, wait, 0)


def _moe_disp(x, mod_l, dest_flat, n_rows):
    B, T, D = x.shape
    tm = min(TOKEN_TILE, T)
    nt = T // tm
    buf0 = jnp.zeros((n_rows, D), f32)
    return pl.pallas_call(
        _moe_disp_kernel,
        grid=(B, nt),
        in_specs=[
            pl.BlockSpec((tm * TOP_K,), lambda b, i: (b * nt + i,), memory_space=pltpu.SMEM),
            pl.BlockSpec((None, tm, D), lambda b, i: (b, i, 0)),
            pl.BlockSpec((None, 6, D), lambda b, i: (b, 0, 0)),
            pl.BlockSpec(memory_space=pl.ANY),
        ],
        out_specs=pl.BlockSpec(memory_space=pl.ANY),
        out_shape=jax.ShapeDtypeStruct((n_rows, D), f32),
        scratch_shapes=[pltpu.VMEM((tm, D), f32), pltpu.SemaphoreType.DMA(())],
        input_output_aliases={3: 0},
        compiler_params=_cp(("arbitrary", "arbitrary")),
        name="moe_disp",
    )(dest_flat, x, mod_l, buf0)


def _moe_exp_kernel(be_ref, nu_ref, x_ref, wg_ref, wu_ref, wd_ref, o_ref):
    del be_ref
    i = pl.program_id(0)

    @pl.when(i < nu_ref[0])
    def _():
        x = x_ref[...].astype(bf16)
        g = _dot(x, wg_ref[...].astype(bf16))
        u = _dot(x, wu_ref[...].astype(bf16))
        hm = g * _sigmoid(g) * u
        o_ref[...] = _dot(hm.astype(bf16), wd_ref[...].astype(bf16))

    @pl.when(i >= nu_ref[0])
    def _():
        o_ref[...] = jnp.zeros_like(o_ref)


def _moe_exp(buf, block_expert, n_used, w_gate, w_up, w_down):
    n_rows, D = buf.shape
    bm = EXPERT_ROWS
    nb = n_rows // bm
    Fd = w_gate.shape[2]
    return pl.pallas_call(
        _moe_exp_kernel,
        grid_spec=pltpu.PrefetchScalarGridSpec(
            num_scalar_prefetch=2,
            grid=(nb,),
            in_specs=[
                pl.BlockSpec((bm, D), lambda i, be, nu: (i, 0)),
                pl.BlockSpec((None, D, Fd), lambda i, be, nu: (be[i], 0, 0)),
                pl.BlockSpec((None, D, Fd), lambda i, be, nu: (be[i], 0, 0)),
                pl.BlockSpec((None, Fd, D), lambda i, be, nu: (be[i], 0, 0)),
            ],
            out_specs=pl.BlockSpec((bm, D), lambda i, be, nu: (i, 0)),
        ),
        out_shape=jax.ShapeDtypeStruct((n_rows, D), f32),
        compiler_params=_cp(("arbitrary",)),
        name="moe_exp",
    )(block_expert, n_used, buf, w_gate, w_up, w_down)


def _moe_comb_kernel(dest_ref, eout_ref, gw_ref, x_ref, mod_ref, sg_ref, su_ref, sd_ref, ln_ref,
                     o_ref, gath_ref, sem):
    tm = x_ref.shape[0]

    def row_copy(t, kk):
        d = dest_ref[t * TOP_K + kk]
        return pltpu.make_async_copy(eout_ref.at[pl.ds(d, 1), :], gath_ref.at[kk, pl.ds(t, 1), :], sem)

    def start(t, c):
        for kk in range(TOP_K):
            row_copy(t, kk).start()
        return c

    def wait(t, c):
        for kk in range(TOP_K):
            row_copy(t, kk).wait()
        return c

    lax.fori_loop(0, tm, start, 0)
    x = x_ref[...]
    h = x * (1.0 + mod_ref[4:5, :]) + mod_ref[3:4, :]
    hb = h.astype(bf16)
    g = _dot(hb, sg_ref[...])
    u = _dot(hb, su_ref[...])
    y = _dot((g * _sigmoid(g) * u).astype(bf16), sd_ref[...])
    lax.fori_loop(0, tm, wait, 0)
    gw = gw_ref[...]
    for kk in range(TOP_K):
        y = y + gath_ref[kk] * gw[:, kk:kk + 1]
    xn = ALPHA * x + (1.0 + mod_ref[5:6, :]) * y
    o_ref[...] = _layer_norm(xn, ln_ref[0:1, :], ln_ref[1:2, :])


def _moe_comb(eout, dest_flat, gw, x, mod_l, s_gate, s_up, s_down, ln):
    B, T, D = x.shape
    tm = min(TOKEN_TILE, T)
    nt = T // tm
    tok = pl.BlockSpec((None, tm, D), lambda b, i: (b, i, 0))
    return pl.pallas_call(
        _moe_comb_kernel,
        grid=(B, nt),
        in_specs=[
            pl.BlockSpec((tm * TOP_K,), lambda b, i: (b * nt + i,), memory_space=pltpu.SMEM),
            pl.BlockSpec(memory_space=pl.ANY),
            pl.BlockSpec((None, tm, TOP_K), lambda b, i: (b, i, 0)),
            tok,
            pl.BlockSpec((None, 6, D), lambda b, i: (b, 0, 0)),
            _full(s_gate.shape), _full(s_up.shape), _full(s_down.shape), _full(ln.shape),
        ],
        out_specs=tok,
        out_shape=jax.ShapeDtypeStruct((B, T, D), f32),
        scratch_shapes=[pltpu.VMEM((TOP_K, tm, D), f32), pltpu.SemaphoreType.DMA(())],
        compiler_params=_cp(("arbitrary", "arbitrary")),
        name="moe_comb",
    )(dest_flat, eout, gw, x, mod_l, s_gate, s_up, s_down, ln)


def _moe(x, mod_l, router, bias, w_gate, w_up, w_down, s_gate, s_up, s_down, ln):
    B, T, D = x.shape
    n_assign = B * T * TOP_K
    bm = EXPERT_ROWS
    n_rows = -(-(n_assign + N_EXPERTS * (bm - 1)) // bm) * bm
    nb = n_rows // bm
    idx, gw, rank, counts = _moe_route(x, mod_l, router, bias.reshape(1, N_EXPERTS))
    counts = counts.reshape(N_EXPERTS)
    padded = (counts + bm - 1) // bm * bm
    pad_end = jnp.cumsum(padded)
    pad_start = pad_end - padded
    dest = (pad_start[idx] + rank).reshape(n_assign)
    block_expert = jnp.minimum(
        jnp.searchsorted(pad_end, jnp.arange(nb, dtype=i32) * bm, side="right"), N_EXPERTS - 1
    ).astype(i32)
    n_used = (pad_end[-1:] // bm).astype(i32)
    buf = _moe_disp(x, mod_l, dest, n_rows)
    eout = _moe_exp(buf, block_expert, n_used, w_gate, w_up, w_down)
    return _moe_comb(eout, dest, gw, x, mod_l, s_gate.astype(bf16), s_up.astype(bf16),
                     s_down.astype(bf16), ln)


def _fox_pre_kernel(x_ref, mod_ref, wk_ref, wv_ref, wf_ref, wq_ref, wg_ref, vec_ref, fb_ref,
                    eh_ref, eht_ref, q_o, k_o, v_o, gate_o, f_o, carry_ref):
    tm = x_ref.shape[0]

    @pl.when(pl.program_id(1) == 0)
    def _():
        carry_ref[...] = jnp.zeros_like(carry_ref)

    eh = eh_ref[...]
    eht = eht_ref[...]
    x = x_ref[...]
    xb = x.astype(bf16)

    def rms(z, gvec):
        ms = _segsum(z * z, eh, eht) * (1.0 / HEAD_DIM)
        return z * lax.rsqrt(ms + QK_EPS) * gvec

    k_o[...] = rms(_dot(xb, wk_ref[...]), vec_ref[0:1, :]).astype(bf16)
    v_o[...] = _dot(xb, wv_ref[...]).astype(bf16)
    xh, xl = _split2(x)
    wfh, wfl = _split2(wf_ref[...])
    fl = _dot(xh, wfh) + _dot(xh, wfl) + _dot(xl, wfh) + fb_ref[...]
    log_f = jnp.minimum(fl, 0.0) - jnp.log(1.0 + jnp.exp(-jnp.abs(fl)))
    tri = (lax.broadcasted_iota(i32, (tm, tm), 1) <= lax.broadcasted_iota(i32, (tm, tm), 0)).astype(bf16)
    fc = _dot_exact_lhs(tri, log_f) + carry_ref[...]
    f_o[...] = fc
    carry_ref[...] = fc[tm - 1:tm, :]
    hb = (x * (1.0 + mod_ref[1:2, :]) + mod_ref[0:1, :]).astype(bf16)
    q_o[...] = (rms(_dot(hb, wq_ref[...]), vec_ref[1:2, :]) * SCORE_SCALE).astype(bf16)
    gate_o[...] = _sigmoid(_dot(hb, wg_ref[...])).astype(bf16)


def _fox_pre(x, mod_l, wk, wv, wf, wq, wg, vecs, fb, eh, eht):
    B, T, D = x.shape
    tm = min(TOKEN_TILE, T)
    tok = pl.BlockSpec((None, tm, D), lambda b, i: (b, i, 0))
    return pl.pallas_call(
        _fox_pre_kernel,
        grid=(B, T // tm),
        in_specs=[tok, pl.BlockSpec((None, 6, D), lambda b, i: (b, 0, 0)),
                  _full(wk.shape), _full(wv.shape), _full(wf.shape), _full(wq.shape), _full(wg.shape),
                  _full(vecs.shape), _full(fb.shape), _full(eh.shape), _full(eht.shape)],
        out_specs=[tok, tok, tok, tok, pl.BlockSpec((None, tm, N_HEADS), lambda b, i: (b, i, 0))],
        out_shape=[jax.ShapeDtypeStruct((B, T, D), bf16)] * 4 + [jax.ShapeDtypeStruct((B, T, N_HEADS), f32)],
        scratch_shapes=[pltpu.VMEM((1, N_HEADS), f32)],
        compiler_params=_cp(("arbitrary", "arbitrary")),
        name="fox_pre",
    )(x, mod_l, wk, wv, wf, wq, wg, vecs, fb, eh, eht)


def _fox_attn_kernel(q_ref, k_ref, v_ref, fq_ref, fk_ref, o_ref):
    tq = q_ref.shape[0]
    tk = min(ATT_K, k_ref.shape[0])
    i = pl.program_id(2)
    lane = lax.broadcasted_iota(i32, (tq, ATT_LANES), 1)
    q2 = q_ref[...]
    zero = jnp.zeros_like(q2)
    qs = jnp.concatenate([jnp.where(lane < HEAD_DIM, q2, zero), jnp.where(lane >= HEAD_DIM, q2, zero)], axis=0)
    fq = fq_ref[...]
    fqs = jnp.concatenate([fq[:, 0:1], fq[:, 1:2]], axis=0)
    qpos = i * tq + lax.broadcasted_iota(i32, (2 * tq, tk), 0) % tq
    kloc = lax.broadcasted_iota(i32, (2 * tq, tk), 1)
    top = lax.broadcasted_iota(i32, (2 * tq, tk), 0) < tq

    def step(j, carry, masked):
        m, l, acc = carry
        off = pl.multiple_of(j * tk, tk)
        kj = k_ref[pl.ds(off, tk), :]
        vj = v_ref[pl.ds(off, tk), :]
        fk = fk_ref[:, pl.ds(off, tk)]
        s = _dot_nt(qs, kj) + (fqs - jnp.where(top, fk[0:1, :], fk[1:2, :]))
        if masked:
            s = jnp.where(off + kloc <= qpos, s, -jnp.inf)
        m_new = jnp.maximum(m, jnp.max(s, axis=-1, keepdims=True))
        alpha = jnp.exp(m - m_new)
        p = jnp.exp(s - m_new)
        l = alpha * l + jnp.sum(p, axis=-1, keepdims=True)
        acc = alpha * acc + _dot(p.astype(bf16), vj)
        return m_new, l, acc

    init = (jnp.full((2 * tq, 1), -jnp.inf, f32), jnp.zeros((2 * tq, 1), f32),
            jnp.zeros((2 * tq, ATT_LANES), f32))
    n_full = (i * tq) // tk
    carry = lax.fori_loop(0, n_full, lambda j, c: step(j, c, False), init)
    n_all = (i * tq + tq + tk - 1) // tk
    m, l, acc = lax.fori_loop(n_full, n_all, lambda j, c: step(j, c, True), carry)
    o = acc / l
    o_ref[...] = jnp.where(lane < HEAD_DIM, o[:tq], o[tq:]).astype(o_ref.dtype)


def _fox_attn(q, k, v, fq, fk):
    B, T, D = q.shape
    tq = min(ATT_Q, T)
    npair = D // ATT_LANES
    return pl.pallas_call(
        _fox_attn_kernel,
        grid=(B, npair, T // tq),
        in_specs=[
            pl.BlockSpec((None, tq, ATT_LANES), lambda b, p, i: (b, i, p)),
            pl.BlockSpec((None, T, ATT_LANES), lambda b, p, i: (b, 0, p)),
            pl.BlockSpec((None, T, ATT_LANES), lambda b, p, i: (b, 0, p)),
            pl.BlockSpec((None, None, tq, 2), lambda b, p, i: (b, p, i, 0)),
            pl.BlockSpec((None, None, 2, T), lambda b, p, i: (b, p, 0, 0)),
        ],
        out_specs=pl.BlockSpec((None, tq, ATT_LANES), lambda b, p, i: (b, i, p)),
        out_shape=jax.ShapeDtypeStruct((B, T, D), bf16),
        compiler_params=_cp(("arbitrary", "arbitrary", "arbitrary")),
        name="fox_attn",
    )(q, k, v, fq, fk)


def _fox_post_kernel(o_ref, gate_ref, x_ref, mod_ref, wo_ref, ln_ref, out_ref):
    z = o_ref[...] * gate_ref[...]
    o = _dot(z, wo_ref[...])
    xn = ALPHA * x_ref[...] + (1.0 + mod_ref[2:3, :]) * o
    out_ref[...] = _layer_norm(xn, ln_ref[0:1, :], ln_ref[1:2, :])


def _fox_post(o, gate, x, mod_l, w_o, ln):
    B, T, D = x.shape
    tm = min(TOKEN_TILE, T)
    tok = pl.BlockSpec((None, tm, D), lambda b, i: (b, i, 0))
    return pl.pallas_call(
        _fox_post_kernel,
        grid=(B, T // tm),
        in_specs=[tok, tok, tok, pl.BlockSpec((None, 6, D), lambda b, i: (b, 0, 0)),
                  _full(w_o.shape), _full(ln.shape)],
        out_specs=tok,
        out_shape=jax.ShapeDtypeStruct((B, T, D), f32),
        compiler_params=_cp(("arbitrary", "arbitrary")),
        name="fox_post",
    )(o, gate, x, mod_l, w_o, ln)


def _rwkv_layer(x, mod_l, mu, w_rkv, w_o, w0, w1, w2, a0, a1, a2, g1, g2, k_k, k_a, r_k, lnx_g, lnx_b,
                ln_g, ln_b, eh, eht):
    D = x.shape[-1]
    vec_pre = jnp.stack([w0, a0, k_k, k_a, r_k.reshape(D), jnp.zeros_like(w0), jnp.zeros_like(w0),
                         jnp.zeros_like(w0)])
    r, lw, k, v, kk, b, bonus, g = _rwkv_pre(
        x, mod_l, jnp.pad(mu, ((0, 2), (0, 0))), w_rkv.astype(bf16), w1.astype(bf16), w2.astype(bf16),
        a1.astype(bf16), a2.astype(bf16), g1.astype(bf16), g2.astype(bf16), vec_pre, eh, eht)
    y = _rwkv_rec(r, lw, k, v, kk, b)
    vec_post = jnp.stack([lnx_g, lnx_b, ln_g, ln_b] + [jnp.zeros_like(ln_g)] * 4)
    return _rwkv_post(y, bonus, g, x, mod_l, vec_post, w_o.astype(bf16), eh, eht)


def _fox_layer(x, mod_l, kv_w, f_b, k_norm_g, w_qg, q_norm_g, w_o, ln_g, ln_b, eh, eht):
    B, T, D = x.shape
    vecs = jnp.stack([jnp.tile(k_norm_g, N_HEADS), jnp.tile(q_norm_g, N_HEADS)]
                     + [jnp.zeros((D,), f32)] * 6)
    q, k, v, gate, fc = _fox_pre(
        x, mod_l, kv_w[:, :D].astype(bf16), kv_w[:, D:2 * D].astype(bf16), kv_w[:, 2 * D:],
        w_qg[:, :D].astype(bf16), w_qg[:, D:].astype(bf16), vecs, f_b.reshape(1, N_HEADS), eh, eht)
    npair = D // ATT_LANES
    fq = fc.reshape(B, T, npair, 2).transpose(0, 2, 1, 3)
    fk = fc.reshape(B, T, npair, 2).transpose(0, 2, 3, 1)
    o = _fox_attn(q, k, v, fq, fk)
    ln = jnp.stack([ln_g, ln_b] + [jnp.zeros_like(ln_g)] * 6)
    return _fox_post(o, gate, x, mod_l, w_o.astype(bf16), ln)


def kernel(x, c, ada_w, ada_b, ln_g, ln_b, rwkv_mu, rwkv_w_rkv, rwkv_w_o, rwkv_w0, rwkv_w1, rwkv_w2, rwkv_a0, rwkv_a1, rwkv_a2, rwkv_g1, rwkv_g2, rwkv_k_k, rwkv_k_a, rwkv_r_k, rwkv_lnx_g, rwkv_lnx_b, kv_w, f_b, k_norm_g, fox_w_qg, fox_q_norm_g, fox_w_o, moe_router, moe_router_bias, moe_w_gate, moe_w_up, moe_w_down, shared_w_gate, shared_w_up, shared_w_down):
    depth = ada_w.shape[0]
    n_a = rwkv_mu.shape[0]
    mod = _mod(c, ada_w, ada_b)
    eh, eht = _head_indicator()
    for l in range(depth):
        if l < n_a:
            x = _rwkv_layer(x, mod[l], rwkv_mu[l], rwkv_w_rkv[l], rwkv_w_o[l], rwkv_w0[l], rwkv_w1[l],
                            rwkv_w2[l], rwkv_a0[l], rwkv_a1[l], rwkv_a2[l], rwkv_g1[l], rwkv_g2[l],
                            rwkv_k_k[l], rwkv_k_a[l], rwkv_r_k[l], rwkv_lnx_g[l], rwkv_lnx_b[l],
                            ln_g[l, 0], ln_b[l, 0], eh, eht)
        else:
            j = l - n_a
            x = _fox_layer(x, mod[l], kv_w, f_b, k_norm_g, fox_w_qg[j], fox_q_norm_g[j], fox_w_o[j],
                           ln_g[l, 0], ln_b[l, 0], eh, eht)
        ln = jnp.stack([ln_g[l, 1], ln_b[l, 1]] + [jnp.zeros_like(ln_g[l, 1])] * 6)
        x = _moe(x, mod[l], moe_router[l], moe_router_bias[l], moe_w_gate[l], moe_w_up[l], moe_w_down[l],
                 shared_w_gate[l], shared_w_up[l], shared_w_down[l], ln)
    return x
```

```python
import jax
import jax.numpy as jnp
from jax import lax
from jax.experimental import pallas as pl
from jax.experimental.pallas import tpu as pltpu

f32 = jnp.float32
bf16 = jnp.bfloat16
i32 = jnp.int32

D_MODEL = 1024
HEAD_DIM = 64
N_HEADS = D_MODEL // HEAD_DIM
N_EXPERTS = 256
D_EXPERT = 256
TOP_K = 8
N_GROUPS = 8
TOPK_GROUPS = 4
GROUP_SIZE = N_EXPERTS // N_GROUPS
ROUTE_SCALE = 2.5
DEPTH = 2
GN_EPS = 64e-5
LN_EPS = 1e-5
QK_EPS = 1e-6
SCORE_SCALE = HEAD_DIM ** -0.5
ALPHA = (2.0 * DEPTH) ** 0.25

LANES = 128
TOKEN_TILE = 256
REC_CHUNK = 64
REC_BLOCK = 256
REC_LANES = 256
EXPERT_ROWS = 256
ATT_Q = 512
ATT_K = 512
ATT_LANES = 128
LOG2E = 1.4426950408889634
VMEM_LIMIT = 56 << 20


def _cp(sem):
    return pltpu.CompilerParams(dimension_semantics=sem, vmem_limit_bytes=VMEM_LIMIT)


def _dot(a, b):
    return jnp.dot(a, b, preferred_element_type=f32)


def _dot_nt(a, b):
    return lax.dot_general(a, b, (((1,), (1,)), ((), ())), preferred_element_type=f32)


def _dot_tn(a, b):
    return lax.dot_general(a, b, (((0,), (0,)), ((), ())), preferred_element_type=f32)


def _split2(z):
    hi = z.astype(bf16)
    lo = (z - hi.astype(f32)).astype(bf16)
    return hi, lo


def _split3(z):
    hi = z.astype(bf16)
    r1 = z - hi.astype(f32)
    mid = r1.astype(bf16)
    lo = (r1 - mid.astype(f32)).astype(bf16)
    return hi, mid, lo


def _dot_exact_rhs(z, m):
    hi, mid, lo = _split3(z)
    return _dot(hi, m) + _dot(mid, m) + _dot(lo, m)


def _dot_exact_lhs(m, z):
    hi, mid, lo = _split3(z)
    return _dot(m, hi) + _dot(m, mid) + _dot(m, lo)


def _segsum(z, eh, eht):
    s = _dot_exact_rhs(z, eh)
    return _dot_exact_rhs(s, eht)


def _sigmoid(z):
    return 1.0 / (1.0 + jnp.exp(-z))


def _softplus(z):
    return jnp.maximum(z, 0.0) + jnp.log(1.0 + jnp.exp(-jnp.abs(z)))


def _layer_norm(z, g, b):
    mu = jnp.mean(z, axis=-1, keepdims=True)
    d = z - mu
    var = jnp.mean(d * d, axis=-1, keepdims=True)
    return d * lax.rsqrt(var + LN_EPS) * g + b


def _head_indicator():
    d = jnp.arange(D_MODEL)[:, None] // HEAD_DIM
    j = jnp.arange(LANES)[None, :]
    eh = (d == j).astype(bf16)
    return eh, eh.T


def _full(shape):
    n = len(shape)
    return pl.BlockSpec(shape, lambda *_: (0,) * n)


def _mod_kernel(c_ref, w_ref, b_ref, o_ref):
    c = c_ref[...]
    sc = c * _sigmoid(c)
    o_ref[...] = jnp.dot(sc, w_ref[...], precision=lax.Precision.HIGHEST,
                         preferred_element_type=f32) + b_ref[...]


def _mod(c, ada_w, ada_b):
    L, D, D6 = ada_w.shape
    B = c.shape[0]
    nj = D6 // D
    out = pl.pallas_call(
        _mod_kernel,
        grid=(L, nj),
        in_specs=[
            pl.BlockSpec((B, D), lambda l, j: (0, 0)),
            pl.BlockSpec((None, D, D), lambda l, j: (l, 0, j)),
            pl.BlockSpec((None, 1, D), lambda l, j: (l, 0, j)),
        ],
        out_specs=pl.BlockSpec((None, B, D), lambda l, j: (l, 0, j)),
        out_shape=jax.ShapeDtypeStruct((L, B, D6), f32),
        compiler_params=_cp(("arbitrary", "arbitrary")),
        name="mod",
    )(c, ada_w, ada_b.reshape(L, 1, D6))
    return out.reshape(L, B, nj, D)


def _rwkv_pre_kernel(x_ref, mod_ref, mu_ref, wrkv_ref, w1_ref, w2_ref, a1_ref, a2_ref,
                     g1_ref, g2_ref, vec_ref, eh_ref, eht_ref,
                     r_o, lw_o, k_o, v_o, kk_o, b_o, bonus_o, g_o, prev_ref):
    tm = x_ref.shape[0]

    @pl.when(pl.program_id(1) == 0)
    def _():
        prev_ref[...] = jnp.zeros_like(prev_ref)

    h = x_ref[...] * (1.0 + mod_ref[1:2, :]) + mod_ref[0:1, :]
    row = lax.broadcasted_iota(i32, h.shape, 0)
    hprev = jnp.where(row == 0, prev_ref[...], pltpu.roll(h, 1, axis=0))
    prev_ref[...] = h[tm - 1:tm, :]
    xx = hprev - h

    def mix(j):
        return (h + xx * mu_ref[j:j + 1, :]).astype(bf16)

    r = _dot(mix(0), wrkv_ref[0])
    k = _dot(mix(2), wrkv_ref[1])
    v = _dot(mix(3), wrkv_ref[2])
    w0, a0, k_k, k_a, r_k = (vec_ref[j:j + 1, :] for j in range(5))
    wl = _dot(jnp.tanh(_dot(mix(1), w1_ref[...])).astype(bf16), w2_ref[...])
    w = -_softplus(-(w0 + wl)) - 0.5
    a = _sigmoid(a0 + _dot(_dot(mix(4), a1_ref[...]).astype(bf16), a2_ref[...]))
    g = _dot(_sigmoid(_dot(mix(5), g1_ref[...])).astype(bf16), g2_ref[...])
    eh = eh_ref[...]
    eht = eht_ref[...]
    kkr = k * k_k
    kk = kkr * lax.rsqrt(jnp.maximum(_segsum(kkr * kkr, eh, eht), 1e-24))
    k2 = k * (1.0 + (a - 1.0) * k_a)
    r_o[...] = r
    lw_o[...] = -jnp.exp(w)
    k_o[...] = k2
    v_o[...] = v
    kk_o[...] = kk
    b_o[...] = kk * a
    bonus_o[...] = _segsum(r * k2 * r_k, eh, eht) * v
    g_o[...] = g


def _rwkv_pre(x, mod_l, mu, w_rkv, w1, w2, a1, a2, g1, g2, vecs, eh, eht):
    B, T, D = x.shape
    tm = min(TOKEN_TILE, T)
    tok = pl.BlockSpec((None, tm, D), lambda b, i: (b, i, 0))
    outs = pl.pallas_call(
        _rwkv_pre_kernel,
        grid=(B, T // tm),
        in_specs=[
            tok,
            pl.BlockSpec((None, 6, D), lambda b, i: (b, 0, 0)),
            _full(mu.shape), _full(w_rkv.shape), _full(w1.shape), _full(w2.shape),
            _full(a1.shape), _full(a2.shape), _full(g1.shape), _full(g2.shape),
            _full(vecs.shape), _full(eh.shape), _full(eht.shape),
        ],
        out_specs=[tok] * 8,
        out_shape=[jax.ShapeDtypeStruct((B, T, D), f32)] * 8,
        scratch_shapes=[pltpu.VMEM((1, D), f32)],
        compiler_params=_cp(("arbitrary", "arbitrary")),
        name="rwkv_pre",
    )(x, mod_l, mu, w_rkv, w1, w2, a1, a2, g1, g2, vecs, eh, eht)
    return outs


def _rwkv_rec_kernel(r_ref, lw_ref, k_ref, v_ref, kk_ref, b_ref, y_ref, st_ref):
    C = REC_CHUNK
    W = REC_LANES
    G = W // HEAD_DIM
    n_groups = r_ref.shape[1] // W
    n_chunks = r_ref.shape[0] // C

    @pl.when(pl.program_id(1) == 0)
    def _():
        st_ref[...] = jnp.zeros_like(st_ref)

    bd_exp = (lax.broadcasted_iota(i32, (G * C, W), 0) // C
              == lax.broadcasted_iota(i32, (G * C, W), 1) // HEAD_DIM).astype(bf16)
    mt = lax.broadcasted_iota(i32, (C, G * C), 0)
    mi = lax.broadcasted_iota(i32, (C, G * C), 1) % C
    strict = mi < mt
    incl = mi <= mt
    eye = (mi == mt).astype(f32)
    bd_state = (lax.broadcasted_iota(i32, (W, W), 0) // HEAD_DIM
                == lax.broadcasted_iota(i32, (W, W), 1) // HEAD_DIM)
    tri = (lax.broadcasted_iota(i32, (C, C), 1) <= lax.broadcasted_iota(i32, (C, C), 0)).astype(bf16)

    def expand(z):
        zb = z.astype(bf16)
        return jnp.concatenate([zb] * G, axis=0) * bd_exp

    def one_group(rows, q):
        cols = slice(q * W, (q + 1) * W)
        r = r_ref[rows, cols]
        lw = lw_ref[rows, cols]
        k = k_ref[rows, cols]
        v = v_ref[rows, cols]
        kk = kk_ref[rows, cols]
        b = b_ref[rows, cols]
        cw = _dot_exact_lhs(tri, lw)
        cwe = cw[C - 1:C, :]
        rt = (r * jnp.exp(cw)).astype(bf16)
        at = (-kk * jnp.exp(cw - lw)).astype(bf16)
        en = jnp.exp(-cw)
        ec = jnp.exp(cwe - cw)
        vh = expand(v)
        aa = _dot_nt(jnp.concatenate([at, rt], axis=0),
                     jnp.concatenate([expand(b * en), expand(k * en)], axis=0))
        aab = jnp.where(strict, aa[:C, :G * C], 0.0)
        aak = jnp.where(strict, aa[:C, G * C:], 0.0).astype(bf16)
        arb = jnp.where(incl, aa[C:, :G * C], 0.0).astype(bf16)
        ark = jnp.where(incl, aa[C:, G * C:], 0.0).astype(bf16)
        tinv = eye + aab
        p = aab
        for _ in range(5):
            p = _dot(p.astype(bf16), expand(p))
            tinv = tinv + _dot(tinv.astype(bf16), expand(p))
        w0 = _dot(aak, vh)
        yv = _dot(ark, vh)
        st = st_ref[q]
        stb = st.astype(bf16)
        u = _dot(tinv.astype(bf16), expand(_dot_nt(at, stb) + w0))
        y_ref[rows, cols] = _dot_nt(rt, stb) + _dot(arb, expand(u)) + yv
        upd = _dot_tn(jnp.concatenate([u, v], axis=0).astype(bf16),
                      jnp.concatenate([b * ec, k * ec], axis=0).astype(bf16))
        st_ref[q] = st * jnp.exp(cwe) + jnp.where(bd_state, upd, 0.0)

    def chunk(c, carry):
        rows = pl.ds(pl.multiple_of(c * C, C), C)
        for q in range(n_groups):
            one_group(rows, q)
        return carry

    lax.fori_loop(0, n_chunks, chunk, 0)


def _rwkv_rec(r, lw, k, v, kk, b):
    B, T, D = r.shape
    tb = min(REC_BLOCK, T)
    blk = pl.BlockSpec((None, tb, D), lambda bi, i: (bi, i, 0))
    return pl.pallas_call(
        _rwkv_rec_kernel,
        grid=(B, T // tb),
        in_specs=[blk] * 6,
        out_specs=blk,
        out_shape=jax.ShapeDtypeStruct((B, T, D), f32),
        scratch_shapes=[pltpu.VMEM((D // REC_LANES, REC_LANES, REC_LANES), f32)],
        compiler_params=_cp(("arbitrary", "arbitrary")),
        name="rwkv_rec",
    )(r, lw, k, v, kk, b)


def _rwkv_post_kernel(y_ref, bonus_ref, g_ref, x_ref, mod_ref, vec_ref, wo_ref, eh_ref, eht_ref, o_ref):
    eh = eh_ref[...]
    eht = eht_ref[...]
    y = y_ref[...]
    mean = _segsum(y, eh, eht) * (1.0 / HEAD_DIM)
    d = y - mean
    var = _segsum(d * d, eh, eht) * (1.0 / HEAD_DIM)
    yn = d * lax.rsqrt(var + GN_EPS) * vec_ref[0:1, :] + vec_ref[1:2, :]
    z = (yn + bonus_ref[...]) * g_ref[...]
    o = _dot(z.astype(bf16), wo_ref[...])
    xn = ALPHA * x_ref[...] + (1.0 + mod_ref[2:3, :]) * o
    o_ref[...] = _layer_norm(xn, vec_ref[2:3, :], vec_ref[3:4, :])


def _rwkv_post(y, bonus, g, x, mod_l, vecs, w_o, eh, eht):
    B, T, D = x.shape
    tm = min(TOKEN_TILE, T)
    tok = pl.BlockSpec((None, tm, D), lambda b, i: (b, i, 0))
    return pl.pallas_call(
        _rwkv_post_kernel,
        grid=(B, T // tm),
        in_specs=[tok, tok, tok, tok,
                  pl.BlockSpec((None, 6, D), lambda b, i: (b, 0, 0)),
                  _full(vecs.shape), _full(w_o.shape), _full(eh.shape), _full(eht.shape)],
        out_specs=tok,
        out_shape=jax.ShapeDtypeStruct((B, T, D), f32),
        compiler_params=_cp(("arbitrary", "arbitrary")),
        name="rwkv_post",
    )(y, bonus, g, x, mod_l, vecs, w_o, eh, eht)


def _moe_route_kernel(x_ref, mod_ref, router_ref, bias_ref, idx_o, gw_o, rank_o, cnt_o, carry_ref):
    tm = x_ref.shape[0]
    E = N_EXPERTS

    @pl.when((pl.program_id(0) == 0) & (pl.program_id(1) == 0))
    def _():
        carry_ref[...] = jnp.zeros_like(carry_ref)

    h = x_ref[...] * (1.0 + mod_ref[4:5, :]) + mod_ref[3:4, :]
    logits = jnp.dot(h, router_ref[...], precision=lax.Precision.HIGHEST, preferred_element_type=f32)
    scores = _sigmoid(logits)
    choice = scores + bias_ref[...]
    lane = lax.broadcasted_iota(i32, (tm, E), 1)
    grp = lane // GROUP_SIZE
    neg = -jnp.inf

    def first_argmax(z):
        m = jnp.max(z, axis=-1, keepdims=True)
        idx = jnp.min(jnp.where(z == m, lane, E), axis=-1, keepdims=True)
        return m, idx

    gs = []
    for gi in range(N_GROUPS):
        zg = jnp.where(grp == gi, choice, neg)
        m1, i1 = first_argmax(zg)
        m2 = jnp.max(jnp.where(lane == i1, neg, zg), axis=-1, keepdims=True)
        gs.append(m1 + m2)
    emask = jnp.zeros((tm, E), jnp.bool_)
    for gi in range(N_GROUPS):
        beaten = jnp.zeros((tm, 1), i32)
        for gj in range(N_GROUPS):
            if gj == gi:
                continue
            ahead = (gs[gj] > gs[gi]) | ((gs[gj] == gs[gi]) & (gj < gi))
            beaten = beaten + ahead.astype(i32)
        emask = emask | ((grp == gi) & (beaten < TOPK_GROUPS))
    mc = jnp.where(emask, choice, neg)
    lane8 = lax.broadcasted_iota(i32, (tm, TOP_K), 1)
    idx_all = jnp.zeros((tm, TOP_K), i32)
    sc_all = jnp.zeros((tm, TOP_K), f32)
    onehot = jnp.zeros((tm, E), f32)
    picks = []
    for kk in range(TOP_K):
        _, ik = first_argmax(mc)
        hit = lane == ik
        sk = jnp.sum(jnp.where(hit, scores, 0.0), axis=-1, keepdims=True)
        mc = jnp.where(hit, neg, mc)
        onehot = onehot + hit.astype(f32)
        idx_all = jnp.where(lane8 == kk, ik, idx_all)
        sc_all = jnp.where(lane8 == kk, sk, sc_all)
        picks.append(hit)
    gw_o[...] = sc_all / jnp.sum(sc_all, axis=-1, keepdims=True) * ROUTE_SCALE
    idx_o[...] = idx_all
    tri = (lax.broadcasted_iota(i32, (tm, tm), 1) < lax.broadcasted_iota(i32, (tm, tm), 0)).astype(bf16)
    before = _dot(tri, onehot.astype(bf16)) + carry_ref[...]
    rank_all = jnp.zeros((tm, TOP_K), f32)
    for kk in range(TOP_K):
        rk = jnp.sum(jnp.where(picks[kk], before, 0.0), axis=-1, keepdims=True)
        rank_all = jnp.where(lane8 == kk, rk, rank_all)
    rank_o[...] = rank_all.astype(i32)
    total = carry_ref[...] + jnp.sum(onehot, axis=0, keepdims=True)
    carry_ref[...] = total
    cnt_o[...] = total.astype(i32)


def _moe_route(x, mod_l, router, bias):
    B, T, D = x.shape
    tm = min(TOKEN_TILE, T)
    tok = pl.BlockSpec((None, tm, D), lambda b, i: (b, i, 0))
    sel = pl.BlockSpec((None, tm, TOP_K), lambda b, i: (b, i, 0))
    return pl.pallas_call(
        _moe_route_kernel,
        grid=(B, T // tm),
        in_specs=[tok, pl.BlockSpec((None, 6, D), lambda b, i: (b, 0, 0)),
                  _full(router.shape), _full(bias.shape)],
        out_specs=[sel, sel, sel, _full((1, N_EXPERTS))],
        out_shape=[jax.ShapeDtypeStruct((B, T, TOP_K), i32),
                   jax.ShapeDtypeStruct((B, T, TOP_K), f32),
                   jax.ShapeDtypeStruct((B, T, TOP_K), i32),
                   jax.ShapeDtypeStruct((1, N_EXPERTS), i32)],
        scratch_shapes=[pltpu.VMEM((1, N_EXPERTS), f32)],
        compiler_params=_cp(("arbitrary", "arbitrary")),
        name="moe_route",
    )(x, mod_l, router, bias)


def _moe_disp_kernel(ps_ref, pe_ref, dest_ref, x_ref, mod_ref, buf_o, h_ref, z_ref, sem, zsem):
    tm = x_ref.shape[0]
    bm = EXPERT_ROWS

    def zero_block(first_row):
        return pltpu.make_async_copy(z_ref, buf_o.at[pl.ds(pl.multiple_of(first_row, bm), bm), :], zsem)

    @pl.when((pl.program_id(0) == 0) & (pl.program_id(1) == 0))
    def _():
        z_ref[...] = jnp.zeros_like(z_ref)
        n_used = pe_ref[N_EXPERTS - 1] // bm
        n_blocks = buf_o.shape[0] // bm

        def zstart(e, c):
            @pl.when(pe_ref[e] > ps_ref[e])
            def _():
                zero_block(pe_ref[e] - bm).start()
            return c

        def zwait(e, c):
            @pl.when(pe_ref[e] > ps_ref[e])
            def _():
                zero_block(pe_ref[e] - bm).wait()
            return c

        def tstart(j, c):
            zero_block(j * bm).start()
            return c

        def twait(j, c):
            zero_block(j * bm).wait()
            return c

        lax.fori_loop(0, N_EXPERTS, zstart, 0)
        lax.fori_loop(n_used, n_blocks, tstart, 0)
        lax.fori_loop(0, N_EXPERTS, zwait, 0)
        lax.fori_loop(n_used, n_blocks, twait, 0)

    h_ref[...] = x_ref[...] * (1.0 + mod_ref[4:5, :]) + mod_ref[3:4, :]

    def start(t, c):
        for kk in range(TOP_K):
            d = dest_ref[t * TOP_K + kk]
            pltpu.make_async_copy(h_ref.at[pl.ds(t, 1), :], buf_o.at[pl.ds(d, 1), :], sem).start()
        return c

    lax.fori_loop(0, tm, start, 0)
    for _ in range(TOP_K):
        pltpu.make_async_copy(h_ref, buf_o.at[pl.ds(0, tm), :], sem).wait()


def _moe_disp(x, mod_l, dest_flat, pad_start, pad_end, n_rows):
    B, T, D = x.shape
    tm = min(TOKEN_TILE, T)
    nt = T // tm
    return pl.pallas_call(
        _moe_disp_kernel,
        grid_spec=pltpu.PrefetchScalarGridSpec(
            num_scalar_prefetch=2,
            grid=(B, nt),
            in_specs=[
                pl.BlockSpec((tm * TOP_K,), lambda b, i, ps, pe: (b * nt + i,), memory_space=pltpu.SMEM),
                pl.BlockSpec((None, tm, D), lambda b, i, ps, pe: (b, i, 0)),
                pl.BlockSpec((None, 6, D), lambda b, i, ps, pe: (b, 0, 0)),
            ],
            out_specs=pl.BlockSpec(memory_space=pl.ANY),
            scratch_shapes=[pltpu.VMEM((tm, D), f32), pltpu.VMEM((EXPERT_ROWS, D), f32),
                            pltpu.SemaphoreType.DMA(()), pltpu.SemaphoreType.DMA(())],
        ),
        out_shape=jax.ShapeDtypeStruct((n_rows, D), f32),
        compiler_params=_cp(("arbitrary", "arbitrary")),
        name="moe_disp",
    )(pad_start, pad_end, dest_flat, x, mod_l)


def _moe_exp_kernel(be_ref, nu_ref, x_ref, wg_ref, wu_ref, wd_ref, o_ref, wgb, wub, wdb):
    i = pl.program_id(0)

    @pl.when((i == 0) | (be_ref[i] != be_ref[jnp.maximum(i - 1, 0)]))
    def _():
        wgb[...] = wg_ref[...].astype(bf16)
        wub[...] = wu_ref[...].astype(bf16)
        wdb[...] = wd_ref[...].astype(bf16)

    @pl.when(i < nu_ref[0])
    def _():
        x = x_ref[...].astype(bf16)
        g = _dot(x, wgb[...])
        u = _dot(x, wub[...])
        hm = g * _sigmoid(g) * u
        o_ref[...] = _dot(hm.astype(bf16), wdb[...])

    @pl.when(i >= nu_ref[0])
    def _():
        o_ref[...] = jnp.zeros_like(o_ref)


def _moe_exp(buf, block_expert, n_used, w_gate, w_up, w_down, layer):
    n_rows, D = buf.shape
    bm = EXPERT_ROWS
    nb = n_rows // bm
    Fd = w_gate.shape[-1]
    return pl.pallas_call(
        _moe_exp_kernel,
        grid_spec=pltpu.PrefetchScalarGridSpec(
            num_scalar_prefetch=2,
            grid=(nb,),
            in_specs=[
                pl.BlockSpec((bm, D), lambda i, be, nu: (jnp.minimum(i, nu[0] - 1), 0)),
                pl.BlockSpec((None, None, D, Fd), lambda i, be, nu: (layer, be[i], 0, 0)),
                pl.BlockSpec((None, None, D, Fd), lambda i, be, nu: (layer, be[i], 0, 0)),
                pl.BlockSpec((None, None, Fd, D), lambda i, be, nu: (layer, be[i], 0, 0)),
            ],
            out_specs=pl.BlockSpec((bm, D), lambda i, be, nu: (i, 0)),
            scratch_shapes=[pltpu.VMEM((D, Fd), bf16), pltpu.VMEM((D, Fd), bf16), pltpu.VMEM((Fd, D), bf16)],
        ),
        out_shape=jax.ShapeDtypeStruct((n_rows, D), f32),
        compiler_params=_cp(("arbitrary",)),
        name="moe_exp",
    )(block_expert, n_used, buf, w_gate, w_up, w_down)


def _moe_comb_kernel(dest_ref, eout_ref, gw_ref, x_ref, mod_ref, sg_ref, su_ref, sd_ref, ln_ref,
                     o_ref, gath_ref, sem):
    tm = x_ref.shape[0]

    def start(t, c):
        for kk in range(TOP_K):
            d = dest_ref[t * TOP_K + kk]
            pltpu.make_async_copy(eout_ref.at[pl.ds(d, 1), :], gath_ref.at[kk, pl.ds(t, 1), :], sem).start()
        return c

    lax.fori_loop(0, tm, start, 0)
    x = x_ref[...]
    h = x * (1.0 + mod_ref[4:5, :]) + mod_ref[3:4, :]
    hb = h.astype(bf16)
    g = _dot(hb, sg_ref[...])
    u = _dot(hb, su_ref[...])
    y = _dot((g * _sigmoid(g) * u).astype(bf16), sd_ref[...])
    for kk in range(TOP_K):
        pltpu.make_async_copy(eout_ref.at[pl.ds(0, tm), :], gath_ref.at[kk], sem).wait()
    gw = gw_ref[...]
    for kk in range(TOP_K):
        y = y + gath_ref[kk] * gw[:, kk:kk + 1]
    xn = ALPHA * x + (1.0 + mod_ref[5:6, :]) * y
    o_ref[...] = _layer_norm(xn, ln_ref[0:1, :], ln_ref[1:2, :])


def _moe_comb(eout, dest_flat, gw, x, mod_l, s_gate, s_up, s_down, ln):
    B, T, D = x.shape
    tm = min(TOKEN_TILE, T)
    nt = T // tm
    tok = pl.BlockSpec((None, tm, D), lambda b, i: (b, i, 0))
    return pl.pallas_call(
        _moe_comb_kernel,
        grid=(B, nt),
        in_specs=[
            pl.BlockSpec((tm * TOP_K,), lambda b, i: (b * nt + i,), memory_space=pltpu.SMEM),
            pl.BlockSpec(memory_space=pl.ANY),
            pl.BlockSpec((None, tm, TOP_K), lambda b, i: (b, i, 0)),
            tok,
            pl.BlockSpec((None, 6, D), lambda b, i: (b, 0, 0)),
            _full(s_gate.shape), _full(s_up.shape), _full(s_down.shape), _full(ln.shape),
        ],
        out_specs=tok,
        out_shape=jax.ShapeDtypeStruct((B, T, D), f32),
        scratch_shapes=[pltpu.VMEM((TOP_K, tm, D), f32), pltpu.SemaphoreType.DMA(())],
        compiler_params=_cp(("arbitrary", "arbitrary")),
        name="moe_comb",
    )(dest_flat, eout, gw, x, mod_l, s_gate, s_up, s_down, ln)


def _moe(x, mod_l, router, bias, w_gate, w_up, w_down, s_gate, s_up, s_down, ln, layer):
    B, T, D = x.shape
    n_assign = B * T * TOP_K
    bm = EXPERT_ROWS
    n_rows = -(-(n_assign + N_EXPERTS * (bm - 1)) // bm) * bm
    nb = n_rows // bm
    idx, gw, rank, counts = _moe_route(x, mod_l, router, bias.reshape(1, N_EXPERTS))
    counts = counts.reshape(N_EXPERTS)
    padded = (counts + bm - 1) // bm * bm
    pad_end = jnp.cumsum(padded).astype(i32)
    pad_start = pad_end - padded
    experts = jnp.arange(N_EXPERTS, dtype=i32)
    start_of = jnp.sum(jnp.where(idx[..., None] == experts, pad_start, 0), axis=-1)
    dest = (start_of + rank).reshape(n_assign)
    block_row = jnp.arange(nb, dtype=i32)[:, None] * bm
    block_expert = jnp.minimum(jnp.sum((pad_end[None, :] <= block_row).astype(i32), axis=-1), N_EXPERTS - 1)
    n_used = pad_end[-1:] // bm
    buf = _moe_disp(x, mod_l, dest, pad_start, pad_end, n_rows)
    eout = _moe_exp(buf, block_expert, n_used, w_gate, w_up, w_down, layer)
    return _moe_comb(eout, dest, gw, x, mod_l, s_gate.astype(bf16), s_up.astype(bf16),
                     s_down.astype(bf16), ln)


def _fox_pre_kernel(x_ref, mod_ref, wk_ref, wv_ref, wf_ref, wq_ref, wg_ref, vec_ref, fb_ref,
                    eh_ref, eht_ref, q_o, k_o, v_o, gate_o, fh_o, fm_o, fl_o, carry_ref):
    tm = x_ref.shape[0]

    @pl.when(pl.program_id(1) == 0)
    def _():
        carry_ref[...] = jnp.zeros_like(carry_ref)

    eh = eh_ref[...]
    eht = eht_ref[...]
    x = x_ref[...]
    xb = x.astype(bf16)

    def rms(z, gvec):
        ms = _segsum(z * z, eh, eht) * (1.0 / HEAD_DIM)
        return z * lax.rsqrt(ms + QK_EPS) * gvec

    k_o[...] = rms(_dot(xb, wk_ref[...]), vec_ref[0:1, :]).astype(bf16)
    v_o[...] = _dot(xb, wv_ref[...]).astype(bf16)
    xh, xl = _split2(x)
    wfh, wfl = _split2(wf_ref[...])
    fl = _dot(xh, wfh) + _dot(xh, wfl) + _dot(xl, wfh) + fb_ref[...]
    log_f = jnp.minimum(fl, 0.0) - jnp.log(1.0 + jnp.exp(-jnp.abs(fl)))
    tri = (lax.broadcasted_iota(i32, (tm, tm), 1) <= lax.broadcasted_iota(i32, (tm, tm), 0)).astype(bf16)
    fc = _dot_exact_lhs(tri, log_f) + carry_ref[...]
    carry_ref[...] = fc[tm - 1:tm, :]
    fh_o[...], fm_o[...], fl_o[...] = _split3(fc * LOG2E)
    hb = (x * (1.0 + mod_ref[1:2, :]) + mod_ref[0:1, :]).astype(bf16)
    q_o[...] = (rms(_dot(hb, wq_ref[...]), vec_ref[1:2, :]) * (SCORE_SCALE * LOG2E)).astype(bf16)
    gate_o[...] = _sigmoid(_dot(hb, wg_ref[...])).astype(bf16)


def _fox_pre(x, mod_l, wk, wv, wf, wq, wg, vecs, fb, eh, eht):
    B, T, D = x.shape
    tm = min(TOKEN_TILE, T)
    tok = pl.BlockSpec((None, tm, D), lambda b, i: (b, i, 0))
    return pl.pallas_call(
        _fox_pre_kernel,
        grid=(B, T // tm),
        in_specs=[tok, pl.BlockSpec((None, 6, D), lambda b, i: (b, 0, 0)),
                  _full(wk.shape), _full(wv.shape), _full(wf.shape), _full(wq.shape), _full(wg.shape),
                  _full(vecs.shape), _full(fb.shape), _full(eh.shape), _full(eht.shape)],
        out_specs=[tok] * 4 + [pl.BlockSpec((None, tm, N_HEADS), lambda b, i: (b, i, 0))] * 3,
        out_shape=[jax.ShapeDtypeStruct((B, T, D), bf16)] * 4
        + [jax.ShapeDtypeStruct((B, T, N_HEADS), bf16)] * 3,
        scratch_shapes=[pltpu.VMEM((1, N_HEADS), f32)],
        compiler_params=_cp(("arbitrary", "arbitrary")),
        name="fox_pre",
    )(x, mod_l, wk, wv, wf, wq, wg, vecs, fb, eh, eht)


def _fox_attn_kernel(q_ref, k_ref, v_ref, o_ref):
    tq = q_ref.shape[0]
    tk = min(ATT_K, k_ref.shape[0])
    i = pl.program_id(2)
    q = q_ref[...]
    qpos = i * tq + lax.broadcasted_iota(i32, (tq, tk), 0)
    kloc = lax.broadcasted_iota(i32, (tq, tk), 1)

    def step(j, carry, masked):
        m, acc = carry
        off = pl.multiple_of(j * tk, tk)
        s = _dot_nt(q, k_ref[pl.ds(off, tk), :])
        if masked:
            s = jnp.where(off + kloc <= qpos, s, -jnp.inf)
        m_new = jnp.maximum(m, jnp.max(s, axis=-1, keepdims=True))
        p = jnp.exp2(s - m_new).astype(bf16)
        acc = jnp.exp2(m - m_new) * acc + _dot(p, v_ref[pl.ds(off, tk), :])
        return m_new, acc

    init = (jnp.full((tq, 1), -jnp.inf, f32), jnp.zeros((tq, ATT_LANES), f32))
    n_full = (i * tq) // tk
    carry = lax.fori_loop(0, n_full, lambda j, c: step(j, c, False), init)
    n_all = (i * tq + tq + tk - 1) // tk
    _, acc = lax.fori_loop(n_full, n_all, lambda j, c: step(j, c, True), carry)
    o_ref[...] = (acc / acc[:, HEAD_DIM:HEAD_DIM + 1]).astype(o_ref.dtype)


def _fox_attn(qx, kx, vx):
    B, T, DX = qx.shape
    tq = min(ATT_Q, T)
    return pl.pallas_call(
        _fox_attn_kernel,
        grid=(B, DX // ATT_LANES, T // tq),
        in_specs=[
            pl.BlockSpec((None, tq, ATT_LANES), lambda b, h, i: (b, i, h)),
            pl.BlockSpec((None, T, ATT_LANES), lambda b, h, i: (b, 0, h)),
            pl.BlockSpec((None, T, ATT_LANES), lambda b, h, i: (b, 0, h)),
        ],
        out_specs=pl.BlockSpec((None, tq, ATT_LANES), lambda b, h, i: (b, i, h)),
        out_shape=jax.ShapeDtypeStruct((B, T, DX), bf16),
        compiler_params=_cp(("arbitrary", "arbitrary", "arbitrary")),
        name="fox_attn",
    )(qx, kx, vx)


def _fox_post_kernel(o_ref, gate_ref, x_ref, mod_ref, wo_ref, ln_ref, out_ref):
    z = o_ref[...] * gate_ref[...]
    o = _dot(z, wo_ref[...])
    xn = ALPHA * x_ref[...] + (1.0 + mod_ref[2:3, :]) * o
    out_ref[...] = _layer_norm(xn, ln_ref[0:1, :], ln_ref[1:2, :])


def _fox_post(o, gate, x, mod_l, w_o, ln):
    B, T, D = x.shape
    tm = min(TOKEN_TILE, T)
    tok = pl.BlockSpec((None, tm, D), lambda b, i: (b, i, 0))
    return pl.pallas_call(
        _fox_post_kernel,
        grid=(B, T // tm),
        in_specs=[tok, tok, tok, pl.BlockSpec((None, 6, D), lambda b, i: (b, 0, 0)),
                  _full(w_o.shape), _full(ln.shape)],
        out_specs=tok,
        out_shape=jax.ShapeDtypeStruct((B, T, D), f32),
        compiler_params=_cp(("arbitrary", "arbitrary")),
        name="fox_post",
    )(o, gate, x, mod_l, w_o, ln)


def _rwkv_layer(x, mod_l, mu, w_rkv, w_o, w0, w1, w2, a0, a1, a2, g1, g2, k_k, k_a, r_k, lnx_g, lnx_b,
                ln_g, ln_b, eh, eht):
    D = x.shape[-1]
    vec_pre = jnp.stack([w0, a0, k_k, k_a, r_k.reshape(D), jnp.zeros_like(w0), jnp.zeros_like(w0),
                         jnp.zeros_like(w0)])
    r, lw, k, v, kk, b, bonus, g = _rwkv_pre(
        x, mod_l, jnp.pad(mu, ((0, 2), (0, 0))), w_rkv.astype(bf16), w1.astype(bf16), w2.astype(bf16),
        a1.astype(bf16), a2.astype(bf16), g1.astype(bf16), g2.astype(bf16), vec_pre, eh, eht)
    y = _rwkv_rec(r, lw, k, v, kk, b)
    vec_post = jnp.stack([lnx_g, lnx_b, ln_g, ln_b] + [jnp.zeros_like(ln_g)] * 4)
    return _rwkv_post(y, bonus, g, x, mod_l, vec_post, w_o.astype(bf16), eh, eht)


def _fox_layer(x, mod_l, kv_w, f_b, k_norm_g, w_qg, q_norm_g, w_o, ln_g, ln_b, eh, eht):
    B, T, D = x.shape
    vecs = jnp.stack([jnp.tile(k_norm_g, N_HEADS), jnp.tile(q_norm_g, N_HEADS)]
                     + [jnp.zeros((D,), f32)] * 6)
    q, k, v, gate, f_hi, f_mid, f_lo = _fox_pre(
        x, mod_l, kv_w[:, :D].astype(bf16), kv_w[:, D:2 * D].astype(bf16), kv_w[:, 2 * D:],
        w_qg[:, :D].astype(bf16), w_qg[:, D:].astype(bf16), vecs, f_b.reshape(1, N_HEADS), eh, eht)
    fs = jnp.stack([f_hi, f_mid, f_lo], axis=-1)
    ones = jnp.ones((B, T, N_HEADS, 3), bf16)
    pad = ATT_LANES - HEAD_DIM
    heads = lambda z: z.reshape(B, T, N_HEADS, HEAD_DIM)
    widen = lambda z, extra: jnp.concatenate(
        [heads(z), extra, jnp.zeros((B, T, N_HEADS, pad - extra.shape[-1]), bf16)], axis=-1
    ).reshape(B, T, N_HEADS * ATT_LANES)
    qx = widen(q, jnp.concatenate([fs, ones], axis=-1))
    kx = widen(k, jnp.concatenate([ones, -fs], axis=-1))
    vx = widen(v, ones[..., :1])
    o = _fox_attn(qx, kx, vx).reshape(B, T, N_HEADS, ATT_LANES)[..., :HEAD_DIM].reshape(B, T, D)
    ln = jnp.stack([ln_g, ln_b] + [jnp.zeros_like(ln_g)] * 6)
    return _fox_post(o, gate, x, mod_l, w_o.astype(bf16), ln)


def kernel(x, c, ada_w, ada_b, ln_g, ln_b, rwkv_mu, rwkv_w_rkv, rwkv_w_o, rwkv_w0, rwkv_w1, rwkv_w2, rwkv_a0, rwkv_a1, rwkv_a2, rwkv_g1, rwkv_g2, rwkv_k_k, rwkv_k_a, rwkv_r_k, rwkv_lnx_g, rwkv_lnx_b, kv_w, f_b, k_norm_g, fox_w_qg, fox_q_norm_g, fox_w_o, moe_router, moe_router_bias, moe_w_gate, moe_w_up, moe_w_down, shared_w_gate, shared_w_up, shared_w_down):
    depth = ada_w.shape[0]
    n_a = rwkv_mu.shape[0]
    assert depth - n_a == 1, "one FoX layer consumes the shared K/V stream"
    mod = _mod(c, ada_w, ada_b)
    eh, eht = _head_indicator()
    for l in range(depth):
        if l < n_a:
            x = _rwkv_layer(x, mod[l], rwkv_mu[l], rwkv_w_rkv[l], rwkv_w_o[l], rwkv_w0[l], rwkv_w1[l],
                            rwkv_w2[l], rwkv_a0[l], rwkv_a1[l], rwkv_a2[l], rwkv_g1[l], rwkv_g2[l],
                            rwkv_k_k[l], rwkv_k_a[l], rwkv_r_k[l], rwkv_lnx_g[l], rwkv_lnx_b[l],
                            ln_g[l, 0], ln_b[l, 0], eh, eht)
        else:
            j = l - n_a
            x = _fox_layer(x, mod[l], kv_w, f_b, k_norm_g, fox_w_qg[j], fox_q_norm_g[j], fox_w_o[j],
                           ln_g[l, 0], ln_b[l, 0], eh, eht)
        ln = jnp.stack([ln_g[l, 1], ln_b[l, 1]] + [jnp.zeros_like(ln_g[l, 1])] * 6)
        x = _moe(x, mod[l], moe_router[l], moe_router_bias[l], moe_w_gate, moe_w_up, moe_w_down,
                 shared_w_gate[l], shared_w_up[l], shared_w_down[l], ln, l)
    return x
```
